```python
import jax, jax.numpy as jnp
from jax import lax
import numpy as np

D_MODEL = 1024
BATCH = 8
SEQ = 2048
DEPTH = 1

CONV_WIDTH = D_MODEL // 2
CONV_K = 3
GLA_HEADS = 4
GLA_DV = (D_MODEL // 2) // GLA_HEADS
GLA_DK = GLA_DV // 2
GLA_K = GLA_HEADS * GLA_DK
GLA_V = GLA_HEADS * GLA_DV
GLA_GATE_RANK = 16
GLA_TAU = 16.0
GLA_CHUNK = 64
MIX_WIDTH = CONV_WIDTH + GLA_V
IN_COLS = 3 * CONV_WIDTH + 2 * GLA_K + 2 * GLA_V + GLA_GATE_RANK

PEER_HEADS = 8
PEER_N_KEYS = 128
PEER_N_EXPERTS = PEER_N_KEYS * PEER_N_KEYS
PEER_TOPK = 16
PEER_QDIM = 256
PEER_HALF = PEER_QDIM // 2
PEER_BLOCK = 128

LN_EPS = 1e-5
RMS_EPS = 1e-6

kernel_name = "hybrid_conv_gla_peer_deepnorm_adaln"


def _layernorm(x, g, b):
    xf = x.astype(jnp.float32)
    mu = jnp.mean(xf, axis=-1, keepdims=True)
    var = jnp.mean(jnp.square(xf - mu), axis=-1, keepdims=True)
    return ((xf - mu) * lax.rsqrt(var + LN_EPS) * g.astype(jnp.float32)
            + b.astype(jnp.float32)).astype(x.dtype)


def _short_conv_mixer(gate_b, gate_c, hval, conv_w):
    u = gate_c * hval
    seq = u.shape[1]
    up = jnp.pad(u, ((0, 0), (CONV_K - 1, 0), (0, 0)))
    z = sum(conv_w[j] * up[:, j:j + seq] for j in range(CONV_K))
    return gate_b * z


def _gla_chunked(q, k, v, g):
    bsz, seq, nh, dk = q.shape
    dv = v.shape[-1]
    nc = seq // GLA_CHUNK

    def to_chunks(a):
        return a.reshape(bsz, nc, GLA_CHUNK, nh, a.shape[-1]).transpose(0, 3, 1, 2, 4)

    q, k, v, g = (to_chunks(a.astype(jnp.float32)) for a in (q, k, v, g))
    b = jnp.cumsum(g, axis=3)
    b_last = b[:, :, :, -1:, :]
    qe = q * jnp.exp(b)
    ke = k * jnp.exp(-b)
    kl = k * jnp.exp(b_last - b)

    causal = jnp.tril(jnp.ones((GLA_CHUNK, GLA_CHUNK), dtype=bool))
    attn = jnp.einsum('bhncd,bhnmd->bhncm', qe, ke)
    attn = jnp.where(causal, attn, 0.0)
    o_intra = jnp.einsum('bhncm,bhnme->bhnce', attn, v)

    kv = jnp.einsum('bhncd,bhnce->bhnde', kl, v)
    decay = jnp.exp(b_last[:, :, :, 0, :])

    def step(state, inp):
        kv_n, dec_n = inp
        return dec_n[..., None] * state + kv_n, state

    init = jnp.zeros((bsz, nh, dk, dv), jnp.float32)
    _, s_prev = lax.scan(step, init, (jnp.moveaxis(kv, 2, 0), jnp.moveaxis(decay, 2, 0)))
    s_prev = jnp.moveaxis(s_prev, 0, 2)
    o_inter = jnp.einsum('bhncd,bhnde->bhnce', qe, s_prev)

    o = o_intra + o_inter
    return o.transpose(0, 2, 3, 1, 4).reshape(bsz, seq, nh, dv)


def _peer(h, peer_wq, peer_subkeys, peer_u, peer_v):
    bsz, seq, dm = h.shape
    ntok = bsz * seq
    t = h.reshape(ntok, dm)
    q = (t @ peer_wq).reshape(ntok, PEER_HEADS, 2, PEER_HALF)
    s = jnp.einsum('thpd,hpnd->thpn', q.astype(jnp.float32), peer_subkeys.astype(jnp.float32))
    s1, i1 = lax.top_k(s[:, :, 0], PEER_TOPK)
    s2, i2 = lax.top_k(s[:, :, 1], PEER_TOPK)
    cand = (s1[..., :, None] + s2[..., None, :]).reshape(ntok, PEER_HEADS, PEER_TOPK * PEER_TOPK)
    cidx = (i1[..., :, None] * PEER_N_KEYS + i2[..., None, :]).reshape(ntok, PEER_HEADS, PEER_TOPK * PEER_TOPK)
    top, pos = lax.top_k(cand, PEER_TOPK)
    idx = jnp.take_along_axis(cidx, pos, axis=-1)
    gate = jax.nn.softmax(top, axis=-1).astype(h.dtype)

    nb = ntok // PEER_BLOCK

    def block(args):
        tb, ib, gb = args
        u = peer_u[ib]
        a = jax.nn.gelu(jnp.einsum('thkd,td->thk', u, tb), approximate=False)
        vv = peer_v[ib]
        return jnp.einsum('thk,thkd->td', gb * a, vv)

    out = lax.map(block, (t.reshape(nb, PEER_BLOCK, dm),
                          idx.reshape(nb, PEER_BLOCK, PEER_HEADS, PEER_TOPK),
                          gate.reshape(nb, PEER_BLOCK, PEER_HEADS, PEER_TOPK)))
    return out.reshape(bsz, seq, dm)


def setup_inputs(seed: int = 0) -> dict:
    key = jax.random.key(seed)
    ks = jax.random.split(key, 20)
    beta = (8.0 * DEPTH) ** -0.25
    f32 = jnp.float32
    nrm = lambda k, shape, s: (jax.random.normal(k, shape, f32) * s)
    return {
        "x": nrm(ks[0], (BATCH, SEQ, D_MODEL), 1.0),
        "c": nrm(ks[1], (BATCH, D_MODEL), 1.0),
        "w_mod": nrm(ks[2], (D_MODEL, 6 * D_MODEL), 0.5 * D_MODEL ** -0.5),
        "b_mod": nrm(ks[3], (6 * D_MODEL,), 0.02),
        "w_in": nrm(ks[4], (D_MODEL, IN_COLS), D_MODEL ** -0.5),
        "conv_w": nrm(ks[5], (CONV_K, CONV_WIDTH), CONV_K ** -0.5),
        "gla_gate_up": nrm(ks[6], (GLA_GATE_RANK, GLA_K), GLA_GATE_RANK ** -0.5),
        "gla_gate_bias": nrm(ks[7], (GLA_K,), 0.1),
        "gla_norm_g": 1.0 + nrm(ks[8], (GLA_V,), 0.02),
        "w_out": nrm(ks[9], (MIX_WIDTH, D_MODEL), beta * MIX_WIDTH ** -0.5),
        "ln1_g": 1.0 + nrm(ks[10], (D_MODEL,), 0.02),
        "ln1_b": nrm(ks[11], (D_MODEL,), 0.02),
        "peer_wq": nrm(ks[12], (D_MODEL, PEER_HEADS * PEER_QDIM), D_MODEL ** -0.5),
        "peer_subkeys": nrm(ks[13], (PEER_HEADS, 2, PEER_N_KEYS, PEER_HALF), PEER_HALF ** -0.5),
        "peer_u": nrm(ks[14], (PEER_N_EXPERTS, D_MODEL), D_MODEL ** -0.5),
        "peer_v": nrm(ks[15], (PEER_N_EXPERTS, D_MODEL), beta),
        "ln2_g": 1.0 + nrm(ks[16], (D_MODEL,), 0.02),
        "ln2_b": nrm(ks[17], (D_MODEL,), 0.02),
    }


def reference(x, c, w_mod, b_mod, w_in, conv_w, gla_gate_up, gla_gate_bias, gla_norm_g,
              w_out, ln1_g, ln1_b, peer_wq, peer_subkeys, peer_u, peer_v, ln2_g, ln2_b):
    alpha = (2.0 * DEPTH) ** 0.25
    bsz, seq, _ = x.shape
    for _layer in range(DEPTH):
        mod = jax.nn.silu(c) @ w_mod + b_mod
        shift1, scale1, gate1, shift2, scale2, gate2 = (m[:, None, :] for m in jnp.split(mod, 6, axis=-1))

        h = x * (1.0 + scale1) + shift1
        proj = h @ w_in
        offs = np.cumsum([CONV_WIDTH, CONV_WIDTH, CONV_WIDTH, GLA_K, GLA_K, GLA_V, GLA_V]).tolist()
        cb, cc, ch, q, k, v, r, glr = jnp.split(proj, offs, axis=-1)

        y_conv = _short_conv_mixer(cb, cc, ch, conv_w)

        log_a = jax.nn.log_sigmoid((glr @ gla_gate_up + gla_gate_bias).astype(jnp.float32)) / GLA_TAU
        o = _gla_chunked(q.reshape(bsz, seq, GLA_HEADS, GLA_DK) * (GLA_DK ** -0.5),
                         k.reshape(bsz, seq, GLA_HEADS, GLA_DK),
                         v.reshape(bsz, seq, GLA_HEADS, GLA_DV),
                         log_a.reshape(bsz, seq, GLA_HEADS, GLA_DK))
        o = o * lax.rsqrt(jnp.mean(jnp.square(o), axis=-1, keepdims=True) + RMS_EPS)
        o = o.reshape(bsz, seq, GLA_V) * gla_norm_g.astype(jnp.float32)
        y_gla = (o * jax.nn.silu(r.astype(jnp.float32))).astype(x.dtype)

        y_mix = jnp.concatenate([y_conv, y_gla], axis=-1) @ w_out
        x = _layernorm(alpha * x + (1.0 + gate1) * y_mix, ln1_g, ln1_b)

        h2 = x * (1.0 + scale2) + shift2
        y_ffn = _peer(h2, peer_wq, peer_subkeys, peer_u, peer_v)
        x = _layernorm(alpha * x + (1.0 + gate2) * y_ffn, ln2_g, ln2_b)
    return x
```

```python
import functools
import math

import jax
import jax.numpy as jnp
from jax import lax
from jax.experimental import pallas as pl
from jax.experimental.pallas import tpu as pltpu

F32 = jnp.float32
BF16 = jnp.bfloat16

D_MODEL = 1024
CONV_WIDTH = 512
CONV_K = 3
GLA_HEADS = 4
GLA_DV = 128
GLA_DK = 64
GLA_K = GLA_HEADS * GLA_DK
GLA_V = GLA_HEADS * GLA_DV
GLA_GATE_RANK = 16
GLA_TAU = 16.0
GLA_CHUNK = 64
MAIN_COLS = 3 * CONV_WIDTH + 2 * GLA_K + 2 * GLA_V
GATE_PAD = 128

PEER_HEADS = 8
PEER_N_KEYS = 128
PEER_TOPK = 16
PEER_HALF = 128
PEER_QCOLS = PEER_HEADS * 2 * PEER_HALF

LN_EPS = 1e-5
RMS_EPS = 1e-6
ALPHA = 2.0 ** 0.25

SEQ_TILE = 256
SEL_TILE = 256
TOK_TILE = 256
EXP_TILE = 1024
KEYS_PER_STEP = EXP_TILE // PEER_N_KEYS
PACK_ROWS = 16
VMEM_LIMIT = 56 * 1024 * 1024


def _sigmoid(x):
    return 1.0 / (1.0 + jnp.exp(-x))


def _layernorm(x, g, b):
    mu = jnp.mean(x, axis=-1, keepdims=True)
    xc = x - mu
    var = jnp.mean(xc * xc, axis=-1, keepdims=True)
    return xc * lax.rsqrt(var + LN_EPS) * g + b


def _mod_kernel(c_ref, w_ref, b_ref, o_ref):
    c = c_ref[...]
    o_ref[...] = jnp.dot(c * _sigmoid(c), w_ref[...], preferred_element_type=F32) + b_ref[...]


def _modulation(c, w_mod, b_mod):
    bsz = c.shape[0]
    ncol = w_mod.shape[1]
    return pl.pallas_call(
        _mod_kernel,
        grid=(ncol // D_MODEL,),
        in_specs=[pl.BlockSpec((bsz, D_MODEL), lambda i: (0, 0)),
                  pl.BlockSpec((D_MODEL, D_MODEL), lambda i: (0, i)),
                  pl.BlockSpec((1, D_MODEL), lambda i: (0, i))],
        out_specs=pl.BlockSpec((bsz, D_MODEL), lambda i: (0, i)),
        out_shape=jax.ShapeDtypeStruct((bsz, ncol), F32),
        name="adaln_mod",
    )(c, w_mod, b_mod.reshape(1, ncol))


def _mixer_kernel(x_ref, mod_ref, win_ref, wglr_ref, convw_ref, gup_ref, gbias_ref, gnorm_ref,
                  wout_ref, ln1g_ref, ln1b_ref, x1_ref, h2t_ref, ubuf, state, ymix):
    ts = x_ref.shape[1]

    @pl.when(pl.program_id(1) == 0)
    def _():
        ubuf[0:8, :] = jnp.zeros((8, CONV_WIDTH), F32)
        state[...] = jnp.zeros(state.shape, F32)

    x = x_ref[0]
    mod = mod_ref[0]
    shift1, scale1, gate1 = mod[0:1], mod[1:2], mod[2:3]
    shift2, scale2 = mod[3:4], mod[4:5]

    h = (x * (1.0 + scale1) + shift1).astype(BF16)
    proj = jnp.dot(h, win_ref[...], preferred_element_type=F32)
    glr = jnp.dot(h, wglr_ref[...], preferred_element_type=F32)

    o_cc, o_ch, o_q = CONV_WIDTH, 2 * CONV_WIDTH, 3 * CONV_WIDTH
    o_k, o_v, o_r = o_q + GLA_K, o_q + 2 * GLA_K, o_q + 2 * GLA_K + GLA_V

    u = proj[:, o_cc:o_ch] * proj[:, o_ch:o_q]
    ubuf[8:8 + ts, :] = u
    z = (convw_ref[0:1, :] * ubuf[6:6 + ts, :] + convw_ref[1:2, :] * ubuf[7:7 + ts, :]
         + convw_ref[2:3, :] * u)
    ymix[:, 0:CONV_WIDTH] = (proj[:, 0:CONV_WIDTH] * z).astype(BF16)
    ubuf[0:8, :] = ubuf[ts:ts + 8, :]

    gz = jnp.dot(glr.astype(BF16), gup_ref[...], preferred_element_type=F32) + gbias_ref[...]
    g = (jnp.minimum(gz, 0.0) - jnp.log1p(jnp.exp(-jnp.abs(gz)))) * (1.0 / GLA_TAU)

    row = lax.broadcasted_iota(jnp.int32, (GLA_CHUNK, GLA_CHUNK), 0)
    col = lax.broadcasted_iota(jnp.int32, (GLA_CHUNK, GLA_CHUNK), 1)
    causal = row >= col
    tri = causal.astype(F32)
    nt_dims = (((1,), (1,)), ((), ()))

    for c in range(ts // GLA_CHUNK):
        r0 = c * GLA_CHUNK
        b = jnp.dot(tri, g[r0:r0 + GLA_CHUNK, :], preferred_element_type=F32,
                    precision=lax.Precision.HIGHEST)
        b_last = b[GLA_CHUNK - 1:GLA_CHUNK, :]
        q_c = proj[r0:r0 + GLA_CHUNK, o_q:o_k] * (GLA_DK ** -0.5)
        k_c = proj[r0:r0 + GLA_CHUNK, o_k:o_v]
        qe = (q_c * jnp.exp(b)).astype(BF16)
        ke = (k_c * jnp.exp(-b)).astype(BF16)
        kl = (k_c * jnp.exp(b_last - b)).astype(BF16)
        decay = jnp.exp(b_last)
        for hd in range(GLA_HEADS):
            ks = slice(hd * GLA_DK, (hd + 1) * GLA_DK)
            vs = slice(hd * GLA_DV, (hd + 1) * GLA_DV)
            v_h = proj[r0:r0 + GLA_CHUNK, o_v + hd * GLA_DV:o_v + (hd + 1) * GLA_DV]
            attn = lax.dot_general(qe[:, ks], ke[:, ks], nt_dims, preferred_element_type=F32)
            attn = jnp.where(causal, attn, 0.0).astype(BF16)
            st = state[hd]
            o = (jnp.dot(attn, v_h.astype(BF16), preferred_element_type=F32)
                 + lax.dot_general(qe[:, ks], st.astype(BF16), nt_dims, preferred_element_type=F32))
            kv_t = jnp.dot(v_h.T.astype(BF16), kl[:, ks], preferred_element_type=F32)
            state[hd] = st * decay[:, ks] + kv_t
            o = o * lax.rsqrt(jnp.mean(o * o, axis=-1, keepdims=True) + RMS_EPS)
            r_h = proj[r0:r0 + GLA_CHUNK, o_r + hd * GLA_DV:o_r + (hd + 1) * GLA_DV]
            y = o * gnorm_ref[:, vs] * (r_h * _sigmoid(r_h))
            ymix[r0:r0 + GLA_CHUNK, CONV_WIDTH + hd * GLA_DV:CONV_WIDTH + (hd + 1) * GLA_DV] = y.astype(BF16)

    y_mix = jnp.dot(ymix[...], wout_ref[...], preferred_element_type=F32)
    x1 = _layernorm(ALPHA * x + (1.0 + gate1) * y_mix, ln1g_ref[...], ln1b_ref[...])
    x1_ref[0] = x1
    h2 = x1 * (1.0 + scale2) + shift2
    h2t_ref[...] = h2.T.astype(BF16)


def _mixer(x, mod3, w_main, w_glr, conv_w, gate_up, gate_bias, norm_g, w_out, ln1_g, ln1_b):
    bsz, seq, _ = x.shape
    nblk = seq // SEQ_TILE
    const = lambda shape: pl.BlockSpec(shape, lambda b, j: (0,) * len(shape))
    return pl.pallas_call(
        _mixer_kernel,
        grid=(bsz, nblk),
        in_specs=[pl.BlockSpec((1, SEQ_TILE, D_MODEL), lambda b, j: (b, j, 0)),
                  pl.BlockSpec((1, 6, D_MODEL), lambda b, j: (b, 0, 0)),
                  const((D_MODEL, MAIN_COLS)), const((D_MODEL, GATE_PAD)),
                  const((CONV_K, CONV_WIDTH)), const((GATE_PAD, GLA_K)), const((1, GLA_K)),
                  const((1, GLA_V)), const((D_MODEL, D_MODEL)), const((1, D_MODEL)), const((1, D_MODEL))],
        out_specs=[pl.BlockSpec((1, SEQ_TILE, D_MODEL), lambda b, j: (b, j, 0)),
                   pl.BlockSpec((D_MODEL, SEQ_TILE), lambda b, j: (0, b * nblk + j))],
        out_shape=[jax.ShapeDtypeStruct((bsz, seq, D_MODEL), F32),
                   jax.ShapeDtypeStruct((D_MODEL, bsz * seq), BF16)],
        scratch_shapes=[pltpu.VMEM((SEQ_TILE + 8, CONV_WIDTH), F32),
                        pltpu.VMEM((GLA_HEADS, GLA_DV, GLA_DK), F32),
                        pltpu.VMEM((SEQ_TILE, D_MODEL), BF16)],
        compiler_params=pltpu.CompilerParams(dimension_semantics=("arbitrary", "arbitrary"),
                                             vmem_limit_bytes=VMEM_LIMIT),
        name="mixer_ln1",
    )(x, mod3, w_main, w_glr, conv_w, gate_up, gate_bias, norm_g, w_out, ln1_g, ln1_b)


def _select_kernel(h2t_ref, wqt_ref, sk_ref, n_ref, c_ref, r2_ref, e2_ref, a_scr, b_scr):
    neg = -jnp.inf
    q_t = jnp.dot(wqt_ref[...], h2t_ref[...], preferred_element_type=F32)
    for hd in range(PEER_HEADS):
        q1 = q_t[(2 * hd) * PEER_HALF:(2 * hd + 1) * PEER_HALF, :].astype(BF16)
        q2 = q_t[(2 * hd + 1) * PEER_HALF:(2 * hd + 2) * PEER_HALF, :].astype(BF16)
        s1 = jnp.dot(sk_ref[2 * hd], q1, preferred_element_type=F32)
        s2 = jnp.dot(sk_ref[2 * hd + 1], q2, preferred_element_type=F32)

        cur = s1
        for r in range(PEER_TOPK):
            m = jnp.max(cur, axis=0, keepdims=True)
            a_scr[r:r + 1, :] = m
            if r + 1 < PEER_TOPK:
                cur = jnp.where(cur == m, neg, cur)
        cur = s2
        rank2 = jnp.full(s2.shape, float(PEER_TOPK), F32)
        for r in range(PEER_TOPK):
            m = jnp.max(cur, axis=0, keepdims=True)
            b_scr[r:r + 1, :] = m
            hit = cur == m
            rank2 = jnp.where(hit, float(r), rank2)
            cur = jnp.where(hit, neg, cur)

        top_a = a_scr[...]
        top_b = b_scr[...]
        a0, b0 = top_a[0:1, :], top_b[0:1, :]
        cand = jnp.concatenate([a0 + top_b]
                               + [top_a[i:i + 1, :] + top_b[0:8, :] for i in range(1, 8)]
                               + [top_a[8:16, :] + b0], axis=0)
        cur = cand
        for r in range(PEER_TOPK):
            thr = jnp.max(cur, axis=0, keepdims=True)
            if r + 1 < PEER_TOPK:
                cur = jnp.where(cur == thr, neg, cur)
        z = jnp.sum(jnp.where(cand >= thr, jnp.exp(cand - (a0 + b0)), 0.0), axis=0, keepdims=True)

        n = jnp.zeros(s1.shape, F32)
        for j in range(PEER_TOPK):
            n = n + jnp.where(s1 >= thr - top_b[j:j + 1, :], 1.0, 0.0)
        n_ref[hd] = n
        c_ref[hd] = jnp.exp(s1 - a0) / z
        r2_ref[hd] = rank2.astype(BF16)
        e2_ref[hd] = jnp.exp(s2 - b0).astype(BF16)


def _select(h2t, wq_t, subkeys):
    ntok = h2t.shape[1]
    tile = lambda dt: jax.ShapeDtypeStruct((PEER_HEADS, PEER_N_KEYS, ntok), dt)
    spec = pl.BlockSpec((PEER_HEADS, PEER_N_KEYS, SEL_TILE), lambda t: (0, 0, t))
    return pl.pallas_call(
        _select_kernel,
        grid=(ntok // SEL_TILE,),
        in_specs=[pl.BlockSpec((D_MODEL, SEL_TILE), lambda t: (0, t)),
                  pl.BlockSpec((PEER_QCOLS, D_MODEL), lambda t: (0, 0)),
                  pl.BlockSpec((2 * PEER_HEADS, PEER_N_KEYS, PEER_HALF), lambda t: (0, 0, 0))],
        out_specs=[spec, spec, spec, spec],
        out_shape=[tile(F32), tile(F32), tile(BF16), tile(BF16)],
        scratch_shapes=[pltpu.VMEM((PEER_TOPK, SEL_TILE), F32), pltpu.VMEM((PEER_TOPK, SEL_TILE), F32)],
        compiler_params=pltpu.CompilerParams(dimension_semantics=("arbitrary",),
                                             vmem_limit_bytes=VMEM_LIMIT),
        name="peer_select",
    )(h2t, wq_t, subkeys)


def _experts_kernel(h2t_ref, u_ref, vt_ref, n_ref, c_ref, r2_ref, e2_ref, x1_ref, mod_ref,
                    ln2g_ref, ln2b_ref, out_ref, acc, p_scr):
    e = pl.program_id(1)
    tok = h2t_ref.shape[1]

    @pl.when(e == 0)
    def _():
        acc[...] = jnp.zeros(acc.shape, F32)

    a_t = jnp.dot(u_ref[...], h2t_ref[...], preferred_element_type=F32)
    act = 0.5 * a_t * (1.0 + lax.erf(a_t * math.sqrt(0.5)))
    groups = PEER_N_KEYS // PACK_ROWS
    for k in range(KEYS_PER_STEP):
        w = jnp.zeros((groups, PACK_ROWS, tok), BF16)
        for hd in range(PEER_HEADS):
            n_row = jnp.broadcast_to(n_ref[hd, 0, k:k + 1, :], (PACK_ROWS, tok)).astype(BF16)
            c_row = jnp.broadcast_to(c_ref[hd, 0, k:k + 1, :], (PACK_ROWS, tok)).astype(BF16)
            w = w + jnp.where(r2_ref[hd] < n_row[None], e2_ref[hd] * c_row[None], jnp.zeros((), BF16))
        g_k = act[k * PEER_N_KEYS:(k + 1) * PEER_N_KEYS, :].reshape(groups, PACK_ROWS, tok).astype(BF16)
        p_scr[k * PEER_N_KEYS:(k + 1) * PEER_N_KEYS, :] = (w * g_k).reshape(PEER_N_KEYS, tok)
    acc[...] += jnp.dot(vt_ref[...], p_scr[...], preferred_element_type=F32)

    @pl.when(e == pl.num_programs(1) - 1)
    def _():
        mod = mod_ref[0]
        gate2 = mod[5:6]
        y = acc[...].T
        out_ref[...] = _layernorm(ALPHA * x1_ref[...] + (1.0 + gate2) * y, ln2g_ref[...], ln2b_ref[...])


def _experts(h2t, u_b, vt_b, n, c, r2, e2, x1, mod3, ln2_g, ln2_b, seq):
    ntok = h2t.shape[1]
    nexp = u_b.shape[0]
    n_estep = nexp // EXP_TILE
    groups = PEER_N_KEYS // PACK_ROWS
    tiles_per_seq = seq // TOK_TILE
    n4 = n.reshape(PEER_HEADS, n_estep, KEYS_PER_STEP, ntok)
    c4 = c.reshape(PEER_HEADS, n_estep, KEYS_PER_STEP, ntok)
    r4 = r2.reshape(PEER_HEADS, groups, PACK_ROWS, ntok)
    e4 = e2.reshape(PEER_HEADS, groups, PACK_ROWS, ntok)
    row_spec = pl.BlockSpec((PEER_HEADS, 1, KEYS_PER_STEP, TOK_TILE), lambda t, e: (0, e, 0, t))
    tile_spec = pl.BlockSpec((PEER_HEADS, groups, PACK_ROWS, TOK_TILE), lambda t, e: (0, 0, 0, t))
    return pl.pallas_call(
        _experts_kernel,
        grid=(ntok // TOK_TILE, n_estep),
        in_specs=[pl.BlockSpec((D_MODEL, TOK_TILE), lambda t, e: (0, t)),
                  pl.BlockSpec((EXP_TILE, D_MODEL), lambda t, e: (e, 0)),
                  pl.BlockSpec((D_MODEL, EXP_TILE), lambda t, e: (0, e)),
                  row_spec, row_spec, tile_spec, tile_spec,
                  pl.BlockSpec((TOK_TILE, D_MODEL), lambda t, e: (t, 0)),
                  pl.BlockSpec((1, 6, D_MODEL), lambda t, e: (t // tiles_per_seq, 0, 0)),
                  pl.BlockSpec((1, D_MODEL), lambda t, e: (0, 0)),
                  pl.BlockSpec((1, D_MODEL), lambda t, e: (0, 0))],
        out_specs=pl.BlockSpec((TOK_TILE, D_MODEL), lambda t, e: (t, 0)),
        out_shape=jax.ShapeDtypeStruct((ntok, D_MODEL), F32),
        scratch_shapes=[pltpu.VMEM((D_MODEL, TOK_TILE), F32), pltpu.VMEM((EXP_TILE, TOK_TILE), BF16)],
        compiler_params=pltpu.CompilerParams(dimension_semantics=("arbitrary", "arbitrary"),
                                             vmem_limit_bytes=VMEM_LIMIT),
        name="peer_experts_ln2",
    )(h2t, u_b, vt_b, n4, c4, r4, e4, x1, mod3, ln2_g, ln2_b)


def kernel(x, c, w_mod, b_mod, w_in, conv_w, gla_gate_up, gla_gate_bias, gla_norm_g, w_out, ln1_g, ln1_b,
           peer_wq, peer_subkeys, peer_u, peer_v, ln2_g, ln2_b):
    bsz, seq, dm = x.shape
    assert dm == D_MODEL and seq % SEQ_TILE == 0 and seq % TOK_TILE == 0
    assert w_in.shape[1] == MAIN_COLS + GLA_GATE_RANK

    mod3 = _modulation(c, w_mod, b_mod).reshape(bsz, 6, D_MODEL)

    w_main = w_in[:, :MAIN_COLS].astype(BF16)
    w_glr = jnp.pad(w_in[:, MAIN_COLS:], ((0, 0), (0, GATE_PAD - GLA_GATE_RANK))).astype(BF16)
    gate_up = jnp.pad(gla_gate_up, ((0, GATE_PAD - GLA_GATE_RANK), (0, 0))).astype(BF16)
    x1, h2t = _mixer(x, mod3, w_main, w_glr, conv_w, gate_up, gla_gate_bias.reshape(1, GLA_K),
                     gla_norm_g.reshape(1, GLA_V), w_out.astype(BF16),
                     ln1_g.reshape(1, D_MODEL), ln1_b.reshape(1, D_MODEL))

    wq_t = peer_wq.T.astype(BF16)
    subkeys = peer_subkeys.reshape(2 * PEER_HEADS, PEER_N_KEYS, PEER_HALF).astype(BF16)
    n, cg, r2, e2 = _select(h2t, wq_t, subkeys)

    out = _experts(h2t, peer_u.astype(BF16), peer_v.T.astype(BF16), n, cg, r2, e2,
                   x1.reshape(bsz * seq, D_MODEL), mod3,
                   ln2_g.reshape(1, D_MODEL), ln2_b.reshape(1, D_MODEL), seq)
    return out.reshape(bsz, seq, D_MODEL)
```
